```python
import math
import jax, jax.numpy as jnp
from jax import lax
import numpy as np

D_MODEL = 1024
BATCH = 2
SEQ = 8192
DEPTH = 4
DEC_BATCH = 128
DEC_SEQ = 4
PAST_LEN = 8192
PAGE_SIZE = 128

N_MIXERS = 2
N_HEADS = 16
N_KV_HEADS = 4
HEAD_DIM = D_MODEL // N_HEADS
GROUP = N_HEADS // N_KV_HEADS
QKV_DIM = (N_HEADS + 2 * N_KV_HEADS) * HEAD_DIM
WINDOW = 128
CONV_WIDTH = 3
CONV_DIM = D_MODEL
N_MEM = 256
MEM_HEADS = 4
MEM_HEAD_DIM = D_MODEL // MEM_HEADS
PEER_HEADS = 8
N_KEYS = 128
N_EXPERTS = N_KEYS * N_KEYS
PEER_TOPK = 16
PEER_QDIM = 256
PEER_HALF = PEER_QDIM // 2
PEER_BLOCK = 128
N_ATTN_LAYERS = (DEPTH + 1) // 2
N_CONV_LAYERS = DEPTH // 2
DN_ALPHA = (2.0 * DEPTH) ** 0.25
DN_BETA = (8.0 * DEPTH) ** -0.25
LN_EPS = 1e-5
NEG = -1e30

kernel_name = "hybrid_swa_sink_shortconv_peer_decoder_step"


def layer_norm(x, g, b):
    xf = x.astype(jnp.float32)
    mu = jnp.mean(xf, -1, keepdims=True)
    var = jnp.mean(jnp.square(xf - mu), -1, keepdims=True)
    return ((xf - mu) * lax.rsqrt(var + LN_EPS) * g.astype(jnp.float32) + b.astype(jnp.float32)).astype(x.dtype)


def deepnorm(x, fx, g, b):
    return layer_norm(DN_ALPHA * x + fx, g, b)


def split_qkv(x, w_qkv):
    B, L, _ = x.shape
    qkv = x @ w_qkv
    nq = N_HEADS * HEAD_DIM
    nk = N_KV_HEADS * HEAD_DIM
    q = qkv[..., :nq].reshape(B, L, N_KV_HEADS, GROUP, HEAD_DIM)
    k = qkv[..., nq:nq + nk].reshape(B, L, N_KV_HEADS, HEAD_DIM)
    v = qkv[..., nq + nk:].reshape(B, L, N_KV_HEADS, HEAD_DIM)
    return q, k, v


def sink_attention(q, k, v, mask, sinks):
    s = jnp.einsum("...qkgd,...skd->...kgqs", q, k).astype(jnp.float32) * (HEAD_DIM ** -0.5)
    s = jnp.where(mask, s, NEG)
    sk = sinks.astype(jnp.float32).reshape(N_KV_HEADS, GROUP)[:, :, None, None]
    m = jnp.maximum(jnp.max(s, -1, keepdims=True), sk)
    p = jnp.exp(s - m)
    p = (p / (jnp.sum(p, -1, keepdims=True) + jnp.exp(sk - m))).astype(v.dtype)
    return jnp.einsum("...kgqs,...skd->...qkgd", p, v)


def window_attn_prompt(x, w_qkv, sinks, w_o):
    B, L, _ = x.shape
    nb = L // WINDOW
    q, k, v = split_qkv(x, w_qkv)
    qb = q.reshape(B, nb, WINDOW, N_KV_HEADS, GROUP, HEAD_DIM)
    kb = k.reshape(B, nb, WINDOW, N_KV_HEADS, HEAD_DIM)
    vb = v.reshape(B, nb, WINDOW, N_KV_HEADS, HEAD_DIM)
    pad = ((0, 0), (1, 0), (0, 0), (0, 0), (0, 0))
    kw = jnp.concatenate([jnp.pad(kb, pad)[:, :-1], kb], axis=2)
    vw = jnp.concatenate([jnp.pad(vb, pad)[:, :-1], vb], axis=2)
    blk = jnp.arange(nb)[:, None] * WINDOW
    qpos = blk + jnp.arange(WINDOW)[None]
    kpos = blk - WINDOW + jnp.arange(2 * WINDOW)[None]
    d = qpos[:, :, None] - kpos[:, None, :]
    mask = (d >= 0) & (d <= WINDOW) & (kpos[:, None, :] >= 0)
    o = sink_attention(qb, kw, vw, mask[:, None, None], sinks)
    y = o.reshape(B, L, N_HEADS * HEAD_DIM) @ w_o
    wc = min(WINDOW, L)
    return y, k[:, L - wc:], v[:, L - wc:]


def window_attn_sample(x, ck, cv, w_qkv, sinks, w_o):
    B, L, _ = x.shape
    P = ck.shape[1]
    q, k, v = split_qkv(x, w_qkv)
    kk = jnp.concatenate([ck, k.astype(ck.dtype)], axis=1)
    vv = jnp.concatenate([cv, v.astype(cv.dtype)], axis=1)
    qpos = PAST_LEN + jnp.arange(L)
    kpos = PAST_LEN - P + jnp.arange(P + L)
    d = qpos[:, None] - kpos[None, :]
    mask = (d >= 0) & (d <= WINDOW)
    o = sink_attention(q, kk, vv, mask, sinks)
    y = o.reshape(B, L, N_HEADS * HEAD_DIM) @ w_o
    return y, kk[:, L:], vv[:, L:]


def short_conv(x, state, w_in, w_conv, w_out):
    L = x.shape[1]
    gb, gc, h = jnp.split(x @ w_in, 3, axis=-1)
    u = gc * h
    up = jnp.concatenate([state.astype(u.dtype), u], axis=1)
    z = w_conv[0] * up[:, 0:L]
    for j in range(1, CONV_WIDTH):
        z = z + w_conv[j] * up[:, j:j + L]
    return (gb * z) @ w_out, up[:, L:]


def mem_kv(mem, w_kv):
    B, M, _ = mem.shape
    k, v = jnp.split(mem @ w_kv, 2, axis=-1)
    return k.reshape(B, M, MEM_HEADS, MEM_HEAD_DIM), v.reshape(B, M, MEM_HEADS, MEM_HEAD_DIM)


def mem_attn(x, mk, mv, w_q, w_o):
    B, L, _ = x.shape
    q = (x @ w_q).reshape(B, L, MEM_HEADS, MEM_HEAD_DIM)
    s = jnp.einsum("blhd,bmhd->bhlm", q, mk.astype(q.dtype)).astype(jnp.float32) * (MEM_HEAD_DIM ** -0.5)
    p = jax.nn.softmax(s, axis=-1).astype(x.dtype)
    o = jnp.einsum("bhlm,bmhd->blhd", p, mv.astype(x.dtype))
    return o.reshape(B, L, D_MODEL) @ w_o


def peer_block(xb, w_q, sub_keys, u_tab, v_tab):
    T = xb.shape[0]
    q = (xb @ w_q).reshape(T, PEER_HEADS, 2, PEER_HALF)
    s = jnp.einsum("thcd,cnd->thcn", q, sub_keys).astype(jnp.float32)
    sv, si = lax.top_k(s, PEER_TOPK)
    comb = (sv[:, :, 0, :, None] + sv[:, :, 1, None, :]).reshape(T, PEER_HEADS, PEER_TOPK * PEER_TOPK)
    cs, ci = lax.top_k(comb, PEER_TOPK)
    i1 = jnp.take_along_axis(si[:, :, 0], ci // PEER_TOPK, axis=-1)
    i2 = jnp.take_along_axis(si[:, :, 1], ci % PEER_TOPK, axis=-1)
    eid = i1 * N_KEYS + i2
    g = jax.nn.softmax(cs, axis=-1)
    u = u_tab[eid]
    v = v_tab[eid]
    a = jax.nn.gelu(jnp.einsum("td,thkd->thk", xb, u).astype(jnp.float32), approximate=False)
    return jnp.einsum("thk,thkd->td", (g * a).astype(v.dtype), v)


def peer(x, w_q, sub_keys, u_tab, v_tab):
    B, L, D = x.shape
    T = B * L
    nblk = -(-T // PEER_BLOCK)
    xf = jnp.pad(x.reshape(T, D), ((0, nblk * PEER_BLOCK - T), (0, 0)))
    yb = lax.map(lambda xb: peer_block(xb, w_q, sub_keys, u_tab, v_tab), xf.reshape(nblk, PEER_BLOCK, D))
    return yb.reshape(nblk * PEER_BLOCK, D)[:T].reshape(B, L, D)


def setup_inputs(seed: int = 0) -> dict:
    key = jax.random.key(seed)
    ks = jax.random.split(key, 24)

    def nrm(k, shape, scale):
        return jax.random.normal(k, shape, jnp.float32) * scale

    wc = min(WINDOW, PAST_LEN)
    dm = D_MODEL ** -0.5
    return {
        "x_prompt": nrm(ks[0], (BATCH, SEQ, D_MODEL), 1.0),
        "x_sample": nrm(ks[1], (DEC_BATCH, DEC_SEQ, D_MODEL), 1.0),
        "cache_win_k": nrm(ks[2], (N_ATTN_LAYERS, DEC_BATCH, wc, N_KV_HEADS, HEAD_DIM), 1.0),
        "cache_win_v": nrm(ks[3], (N_ATTN_LAYERS, DEC_BATCH, wc, N_KV_HEADS, HEAD_DIM), 1.0),
        "state_conv": nrm(ks[4], (N_CONV_LAYERS, DEC_BATCH, CONV_WIDTH - 1, CONV_DIM), 1.0),
        "cache_mem_k": nrm(ks[5], (DEPTH, DEC_BATCH, N_MEM, MEM_HEADS, MEM_HEAD_DIM), 1.0),
        "cache_mem_v": nrm(ks[6], (DEPTH, DEC_BATCH, N_MEM, MEM_HEADS, MEM_HEAD_DIM), 1.0),
        "mem_prompt": nrm(ks[7], (BATCH, N_MEM, D_MODEL), 1.0),
        "attn_w_qkv": nrm(ks[8], (N_ATTN_LAYERS, D_MODEL, QKV_DIM), dm),
        "attn_sinks": nrm(ks[9], (N_ATTN_LAYERS, N_HEADS), 0.5),
        "attn_w_o": nrm(ks[10], (N_ATTN_LAYERS, N_HEADS * HEAD_DIM, D_MODEL), (N_HEADS * HEAD_DIM) ** -0.5 * DN_BETA),
        "conv_w_in": nrm(ks[11], (N_CONV_LAYERS, D_MODEL, 3 * CONV_DIM), dm),
        "conv_w": nrm(ks[12], (N_CONV_LAYERS, CONV_WIDTH, CONV_DIM), CONV_WIDTH ** -0.5),
        "conv_w_out": nrm(ks[13], (N_CONV_LAYERS, CONV_DIM, D_MODEL), CONV_DIM ** -0.5 * DN_BETA),
        "mem_w_q": nrm(ks[14], (DEPTH, D_MODEL, D_MODEL), dm),
        "mem_w_kv": nrm(ks[15], (DEPTH, D_MODEL, 2 * D_MODEL), dm),
        "mem_w_o": nrm(ks[16], (DEPTH, D_MODEL, D_MODEL), dm * DN_BETA),
        "peer_w_q": nrm(ks[17], (DEPTH, D_MODEL, PEER_HEADS * PEER_QDIM), dm),
        "peer_sub_keys": nrm(ks[18], (DEPTH, 2, N_KEYS, PEER_HALF), PEER_HALF ** -0.5),
        "peer_u": nrm(ks[19], (DEPTH, N_EXPERTS, D_MODEL), dm),
        "peer_v": nrm(ks[20], (DEPTH, N_EXPERTS, D_MODEL), DN_BETA * PEER_HEADS ** -0.5),
        "ln_g": 1.0 + nrm(ks[21], (DEPTH, 3, D_MODEL), 0.02),
        "ln_b": nrm(ks[22], (DEPTH, 3, D_MODEL), 0.02),
    }


def reference(x_prompt, x_sample, cache_win_k, cache_win_v, state_conv, cache_mem_k, cache_mem_v, mem_prompt,
              attn_w_qkv, attn_sinks, attn_w_o, conv_w_in, conv_w, conv_w_out, mem_w_q, mem_w_kv, mem_w_o,
              peer_w_q, peer_sub_keys, peer_u, peer_v, ln_g, ln_b):
    xp, xs = x_prompt, x_sample
    wk_p, wv_p, cv_p, mk_p_all, mv_p_all = [], [], [], [], []
    wk_s, wv_s, cv_s = [], [], []
    for i in range(DEPTH):
        j = i // N_MIXERS
        if i % N_MIXERS == 0:
            fp, kp, vp = window_attn_prompt(xp, attn_w_qkv[j], attn_sinks[j], attn_w_o[j])
            fs, ks_, vs_ = window_attn_sample(xs, cache_win_k[j], cache_win_v[j], attn_w_qkv[j], attn_sinks[j], attn_w_o[j])
            wk_p.append(kp); wv_p.append(vp); wk_s.append(ks_); wv_s.append(vs_)
        else:
            zero_state = jnp.zeros((xp.shape[0], CONV_WIDTH - 1, CONV_DIM), xp.dtype)
            fp, sp = short_conv(xp, zero_state, conv_w_in[j], conv_w[j], conv_w_out[j])
            fs, ss = short_conv(xs, state_conv[j], conv_w_in[j], conv_w[j], conv_w_out[j])
            cv_p.append(sp); cv_s.append(ss)
        xp = deepnorm(xp, fp, ln_g[i, 0], ln_b[i, 0])
        xs = deepnorm(xs, fs, ln_g[i, 0], ln_b[i, 0])
        mk, mv = mem_kv(mem_prompt, mem_w_kv[i])
        mk_p_all.append(mk); mv_p_all.append(mv)
        xp = deepnorm(xp, mem_attn(xp, mk, mv, mem_w_q[i], mem_w_o[i]), ln_g[i, 1], ln_b[i, 1])
        xs = deepnorm(xs, mem_attn(xs, cache_mem_k[i], cache_mem_v[i], mem_w_q[i], mem_w_o[i]), ln_g[i, 1], ln_b[i, 1])
        xp = deepnorm(xp, peer(xp, peer_w_q[i], peer_sub_keys[i], peer_u[i], peer_v[i]), ln_g[i, 2], ln_b[i, 2])
        xs = deepnorm(xs, peer(xs, peer_w_q[i], peer_sub_keys[i], peer_u[i], peer_v[i]), ln_g[i, 2], ln_b[i, 2])
    return (xp, xs,
            jnp.stack(wk_p), jnp.stack(wv_p), jnp.stack(cv_p), jnp.stack(mk_p_all), jnp.stack(mv_p_all),
            jnp.stack(wk_s), jnp.stack(wv_s), jnp.stack(cv_s))
```

```python
import functools
import math

import jax
import jax.numpy as jnp
from jax import lax
from jax.experimental import pallas as pl
from jax.experimental.pallas import tpu as pltpu

D_MODEL = 1024
DEPTH = 4
N_HEADS = 16
N_KV_HEADS = 4
HEAD_DIM = 64
GROUP = 4
WINDOW = 128
N_MEM = 256
MEM_HEADS = 4
MEM_HEAD_DIM = 256
PEER_HEADS = 8
N_KEYS = 128
N_EXPERTS = N_KEYS * N_KEYS
PEER_TOPK = 16
PEER_HALF = 128
DN_ALPHA = (2.0 * DEPTH) ** 0.25
LN_EPS = 1e-5
NEG = -1e30
SQRT_HALF = 0.7071067811865476

BF16 = jnp.bfloat16
F32 = jnp.float32

TOKEN_TILE = 512
EXPERT_TILE = 512
VMEM_LIMIT = 56 * 1024 * 1024


def _params(sem):
    return pltpu.CompilerParams(dimension_semantics=sem, vmem_limit_bytes=VMEM_LIMIT)


def _layer_norm(x, g, b):
    mu = jnp.mean(x, axis=-1, keepdims=True)
    xc = x - mu
    var = jnp.mean(xc * xc, axis=-1, keepdims=True)
    return xc * lax.rsqrt(var + LN_EPS) * g + b


def _dot(a, b):
    return jnp.dot(a, b, preferred_element_type=F32)


def _dot_nt(a, b):
    return lax.dot_general(a, b, (((1,), (1,)), ((), ())), preferred_element_type=F32)


def _linear_kernel(x_ref, w_ref, o_ref):
    o_ref[...] = _dot(x_ref[...].astype(BF16), w_ref[...])


def linear(x, w):
    T, K = x.shape
    N = w.shape[1]
    tm = min(TOKEN_TILE, T)
    return pl.pallas_call(
        _linear_kernel,
        grid=(T // tm,),
        in_specs=[pl.BlockSpec((tm, K), lambda i: (i, 0)),
                  pl.BlockSpec((K, N), lambda i: (0, 0))],
        out_specs=pl.BlockSpec((tm, N), lambda i: (i, 0)),
        out_shape=jax.ShapeDtypeStruct((T, N), F32),
        compiler_params=_params(("parallel",)),
        name="linear",
    )(x, w)


def _linear_dn_kernel(h_ref, w_ref, res_ref, g_ref, b_ref, o_ref):
    f = _dot(h_ref[...].astype(BF16), w_ref[...])
    o_ref[...] = _layer_norm(DN_ALPHA * res_ref[...] + f, g_ref[...], b_ref[...])


def linear_deepnorm(h, w, res, g, b):
    T, K = h.shape
    N = w.shape[1]
    tm = min(TOKEN_TILE, T)
    return pl.pallas_call(
        _linear_dn_kernel,
        grid=(T // tm,),
        in_specs=[pl.BlockSpec((tm, K), lambda i: (i, 0)),
                  pl.BlockSpec((K, N), lambda i: (0, 0)),
                  pl.BlockSpec((tm, N), lambda i: (i, 0)),
                  pl.BlockSpec((1, N), lambda i: (0, 0)),
                  pl.BlockSpec((1, N), lambda i: (0, 0))],
        out_specs=pl.BlockSpec((tm, N), lambda i: (i, 0)),
        out_shape=jax.ShapeDtypeStruct((T, N), F32),
        compiler_params=_params(("parallel",)),
        name="linear_deepnorm",
    )(h, w, res, g, b)


def _win_prompt_kernel(sinks_ref, q_ref, kp_ref, kc_ref, vp_ref, vc_ref, o_ref):
    blk = pl.program_id(1)
    rows = GROUP * WINDOW
    qi = lax.broadcasted_iota(jnp.int32, (rows, 2 * WINDOW), 0) & (WINDOW - 1)
    kj = lax.broadcasted_iota(jnp.int32, (rows, 2 * WINDOW), 1)
    mask = (kj >= qi) & (kj <= qi + WINDOW) & ((kj >= WINDOW) | (blk > 0))
    for kh in range(N_KV_HEADS):
        q = q_ref[0, kh].reshape(rows, HEAD_DIM).astype(BF16)
        kk = jnp.concatenate([kp_ref[0, kh], kc_ref[0, kh]], axis=0).astype(BF16)
        vv = jnp.concatenate([vp_ref[0, kh], vc_ref[0, kh]], axis=0).astype(BF16)
        s = _dot_nt(q, kk) * (HEAD_DIM ** -0.5)
        s = jnp.where(mask, s, NEG)
        for g in range(GROUP):
            sg = s[g * WINDOW:(g + 1) * WINDOW]
            sk = sinks_ref[kh * GROUP + g]
            m = jnp.maximum(jnp.max(sg, axis=-1, keepdims=True), sk)
            p = jnp.exp(sg - m)
            p = p / (jnp.sum(p, axis=-1, keepdims=True) + jnp.exp(sk - m))
            o_ref[0, kh, g] = _dot(p.astype(BF16), vv)


def window_attention_prompt(q, k, v, sinks):
    B, _, _, L, _ = q.shape
    nb = L // WINDOW
    kv_cur = pl.BlockSpec((1, N_KV_HEADS, WINDOW, HEAD_DIM), lambda b, i: (b, 0, i, 0))
    kv_prev = pl.BlockSpec((1, N_KV_HEADS, WINDOW, HEAD_DIM),
                           lambda b, i: (b, 0, jnp.maximum(i - 1, 0), 0))
    q_spec = pl.BlockSpec((1, N_KV_HEADS, GROUP, WINDOW, HEAD_DIM), lambda b, i: (b, 0, 0, i, 0))
    return pl.pallas_call(
        _win_prompt_kernel,
        grid=(B, nb),
        in_specs=[pl.BlockSpec(memory_space=pltpu.SMEM), q_spec, kv_prev, kv_cur, kv_prev, kv_cur],
        out_specs=q_spec,
        out_shape=jax.ShapeDtypeStruct(q.shape, F32),
        compiler_params=_params(("parallel", "parallel")),
        name="window_attention_prompt",
    )(sinks, q, k, k, v, v)


SAMPLE_BATCH_TILE = 8
SAMPLE_KEYS = 2 * WINDOW


def _win_sample_kernel(sk_ref, q_ref, k_ref, v_ref, o_ref, *, n_new):
    nb = SAMPLE_BATCH_TILE * N_KV_HEADS
    rows = GROUP * n_new
    q = q_ref[...].reshape(nb, rows, HEAD_DIM).astype(BF16)
    kk = k_ref[...].reshape(nb, SAMPLE_KEYS, HEAD_DIM).astype(BF16)
    vv = v_ref[...].reshape(nb, SAMPLE_KEYS, HEAD_DIM).astype(BF16)
    s = jnp.einsum("bqd,bkd->bqk", q, kk, preferred_element_type=F32) * (HEAD_DIM ** -0.5)
    ql = lax.broadcasted_iota(jnp.int32, s.shape, 1) % n_new
    kj = lax.broadcasted_iota(jnp.int32, s.shape, 2)
    mask = ((kj < WINDOW) & (kj >= ql)) | ((kj >= WINDOW) & (kj - WINDOW <= ql))
    s = jnp.where(mask, s, NEG)
    sk = sk_ref[...]
    m = jnp.maximum(jnp.max(s, axis=-1, keepdims=True), sk)
    p = jnp.exp(s - m)
    p = p / (jnp.sum(p, axis=-1, keepdims=True) + jnp.exp(sk - m))
    o = jnp.einsum("bqk,bkd->bqd", p.astype(BF16), vv, preferred_element_type=F32)
    o_ref[...] = o.reshape(o_ref.shape)


def window_attention_sample(q, k, v, sinks_col, n_new):
    B = q.shape[0]
    bt = SAMPLE_BATCH_TILE
    rows = q.shape[2]
    q_spec = pl.BlockSpec((bt, N_KV_HEADS, rows, HEAD_DIM), lambda b: (b, 0, 0, 0))
    kv_spec = pl.BlockSpec((bt, N_KV_HEADS, SAMPLE_KEYS, HEAD_DIM), lambda b: (b, 0, 0, 0))
    return pl.pallas_call(
        functools.partial(_win_sample_kernel, n_new=n_new),
        grid=(B // bt,),
        in_specs=[pl.BlockSpec(sinks_col.shape, lambda b: (0, 0, 0)), q_spec, kv_spec, kv_spec],
        out_specs=q_spec,
        out_shape=jax.ShapeDtypeStruct(q.shape, F32),
        compiler_params=_params(("parallel",)),
        name="window_attention_sample",
    )(sinks_col, q, k, v)


def _conv_in_kernel(x_ref, w_ref, gb_ref, u_ref):
    y = _dot(x_ref[...].astype(BF16), w_ref[...])
    c = gb_ref.shape[-1]
    gb_ref[...] = y[:, :c]
    u_ref[...] = y[:, c:2 * c] * y[:, 2 * c:]


def conv_in(x, w_in):
    T, K = x.shape
    C = w_in.shape[1] // 3
    tm = min(TOKEN_TILE, T)
    out_spec = pl.BlockSpec((tm, C), lambda i: (i, 0))
    return pl.pallas_call(
        _conv_in_kernel,
        grid=(T // tm,),
        in_specs=[pl.BlockSpec((tm, K), lambda i: (i, 0)),
                  pl.BlockSpec((K, 3 * C), lambda i: (0, 0))],
        out_specs=[out_spec, out_spec],
        out_shape=[jax.ShapeDtypeStruct((T, C), F32)] * 2,
        compiler_params=_params(("parallel",)),
        name="conv_in",
    )(x, w_in)


def _conv_out_kernel(u0_ref, u1_ref, u2_ref, gb_ref, wc_ref, w_ref, res_ref, g_ref, b_ref, o_ref):
    z = wc_ref[0:1, :] * u0_ref[...]
    z = z + wc_ref[1:2, :] * u1_ref[...]
    z = z + wc_ref[2:3, :] * u2_ref[...]
    f = _dot((gb_ref[...] * z).astype(BF16), w_ref[...])
    o_ref[...] = _layer_norm(DN_ALPHA * res_ref[...] + f, g_ref[...], b_ref[...])


def conv_out(u0, u1, u2, gb, wc, w_out, res, g, b):
    T, C = gb.shape
    N = w_out.shape[1]
    tm = min(TOKEN_TILE, T)
    row = pl.BlockSpec((tm, C), lambda i: (i, 0))
    return pl.pallas_call(
        _conv_out_kernel,
        grid=(T // tm,),
        in_specs=[row, row, row, row,
                  pl.BlockSpec(wc.shape, lambda i: (0, 0)),
                  pl.BlockSpec((C, N), lambda i: (0, 0)),
                  pl.BlockSpec((tm, N), lambda i: (i, 0)),
                  pl.BlockSpec((1, N), lambda i: (0, 0)),
                  pl.BlockSpec((1, N), lambda i: (0, 0))],
        out_specs=pl.BlockSpec((tm, N), lambda i: (i, 0)),
        out_shape=jax.ShapeDtypeStruct((T, N), F32),
        compiler_params=_params(("parallel",)),
        name="conv_out",
    )(u0, u1, u2, gb, wc, w_out, res, g, b)


def _mem_attn_kernel(q_ref, k_ref, v_ref, o_ref):
    for bi in range(q_ref.shape[0]):
        for h in range(MEM_HEADS):
            cols = slice(h * MEM_HEAD_DIM, (h + 1) * MEM_HEAD_DIM)
            q = q_ref[bi, :, cols].astype(BF16)
            k = k_ref[bi, :, cols].astype(BF16)
            v = v_ref[bi, :, cols].astype(BF16)
            s = _dot_nt(q, k) * (MEM_HEAD_DIM ** -0.5)
            m = jnp.max(s, axis=-1, keepdims=True)
            p = jnp.exp(s - m)
            p = p / jnp.sum(p, axis=-1, keepdims=True)
            o_ref[bi, :, cols] = _dot(p.astype(BF16), v)


def mem_attention(q, k, v, bt, tl):
    B, L, Dm = q.shape
    M = k.shape[1]
    q_spec = pl.BlockSpec((bt, tl, Dm), lambda b, i: (b, i, 0))
    kv_spec = pl.BlockSpec((bt, M, Dm), lambda b, i: (b, 0, 0))
    return pl.pallas_call(
        _mem_attn_kernel,
        grid=(B // bt, L // tl),
        in_specs=[q_spec, kv_spec, kv_spec],
        out_specs=q_spec,
        out_shape=jax.ShapeDtypeStruct(q.shape, F32),
        compiler_params=_params(("parallel", "parallel")),
        name="mem_attention",
    )(q, k, v)


def _top_values(s, n):
    vals = []
    cur = s
    for _ in range(n):
        m = jnp.max(cur, axis=0, keepdims=True)
        vals.append(m)
        cur = jnp.where(cur == m, -jnp.inf, cur)
    return vals


def _peer_route_kernel(q_ref, keys_ref, th_ref, e1_ref, s2_ref, e2_ref):
    tm = q_ref.shape[0]
    k_iota = lax.broadcasted_iota(jnp.int32, (PEER_TOPK, tm), 0)
    for h in range(PEER_HEADS):
        base = h * 2 * PEER_HALF
        q1 = q_ref[:, base:base + PEER_HALF].astype(BF16)
        q2 = q_ref[:, base + PEER_HALF:base + 2 * PEER_HALF].astype(BF16)
        s1 = _dot_nt(keys_ref[0], q1)
        s2 = _dot_nt(keys_ref[1], q2)
        v1 = _top_values(s1, PEER_TOPK)
        v2 = _top_values(s2, PEER_TOPK)
        v2_rows = jnp.zeros((PEER_TOPK, tm), F32)
        for k in range(PEER_TOPK):
            v2_rows = jnp.where(k_iota == k, v2[k], v2_rows)
        comb = jnp.concatenate([v1[k] + v2_rows for k in range(PEER_TOPK)], axis=0)
        c = _top_values(comb, PEER_TOPK + 1)
        tau = 0.5 * (c[PEER_TOPK - 1] + c[PEER_TOPK])
        m1, m2 = v1[0], v2[0]
        z = jnp.sum(jnp.where(comb > tau, jnp.exp(comb - (m1 + m2)), 0.0), axis=0, keepdims=True)
        th_ref[h] = tau - s1
        e1_ref[h] = jnp.exp(s1 - m1)
        s2_ref[h] = s2
        e2_ref[h] = jnp.exp(s2 - m2) / z


def peer_route(q, keys):
    T = q.shape[0]
    tm = min(TOKEN_TILE, T)
    out_spec = pl.BlockSpec((PEER_HEADS, N_KEYS, tm), lambda i: (0, 0, i))
    out_shape = jax.ShapeDtypeStruct((PEER_HEADS, N_KEYS, T), F32)
    return pl.pallas_call(
        _peer_route_kernel,
        grid=(T // tm,),
        in_specs=[pl.BlockSpec((tm, q.shape[1]), lambda i: (i, 0)),
                  pl.BlockSpec(keys.shape, lambda i: (0, 0, 0))],
        out_specs=[out_spec] * 4,
        out_shape=[out_shape] * 4,
        compiler_params=_params(("parallel",)),
        name="peer_route",
    )(q, keys)


def _peer_dense_kernel(x_ref, u_ref, vt_ref, th_ref, e1_ref, s2_ref, e2_ref, g_ref, b_ref,
                       o_ref, xt_ref, acc_ref):
    e = pl.program_id(1)
    slabs = EXPERT_TILE // N_KEYS

    @pl.when(e == 0)
    def _():
        xt_ref[...] = x_ref[...].T.astype(BF16)
        acc_ref[...] = jnp.zeros_like(acc_ref)

    a_t = _dot(u_ref[...], xt_ref[...])
    z_t = []
    for il in range(slabs):
        i1 = e * slabs + il
        a = a_t[il * N_KEYS:(il + 1) * N_KEYS]
        w = jnp.zeros_like(a)
        for h in range(PEER_HEADS):
            th = th_ref[h, pl.ds(i1, 1), :]
            e1 = e1_ref[h, pl.ds(i1, 1), :]
            w = w + jnp.where(s2_ref[h] > th, e2_ref[h], 0.0) * e1
        gelu = 0.5 * a * (1.0 + lax.erf(a * SQRT_HALF))
        z_t.append((w * gelu).astype(BF16))
    acc_ref[...] += _dot(vt_ref[...], jnp.concatenate(z_t, axis=0))

    @pl.when(e == pl.num_programs(1) - 1)
    def _():
        y = acc_ref[...].T
        o_ref[...] = _layer_norm(DN_ALPHA * x_ref[...] + y, g_ref[...], b_ref[...])


def peer_dense(x, u, vt, th, e1, s2, e2, g, b):
    T, Dm = x.shape
    tm = min(TOKEN_TILE, T)
    te = EXPERT_TILE
    route_spec = pl.BlockSpec((PEER_HEADS, N_KEYS, tm), lambda i, e: (0, 0, i))
    return pl.pallas_call(
        _peer_dense_kernel,
        grid=(T // tm, N_EXPERTS // te),
        in_specs=[pl.BlockSpec((tm, Dm), lambda i, e: (i, 0)),
                  pl.BlockSpec((te, Dm), lambda i, e: (e, 0)),
                  pl.BlockSpec((Dm, te), lambda i, e: (0, e)),
                  route_spec, route_spec, route_spec, route_spec,
                  pl.BlockSpec((1, Dm), lambda i, e: (0, 0)),
                  pl.BlockSpec((1, Dm), lambda i, e: (0, 0))],
        out_specs=pl.BlockSpec((tm, Dm), lambda i, e: (i, 0)),
        out_shape=jax.ShapeDtypeStruct((T, Dm), F32),
        scratch_shapes=[pltpu.VMEM((Dm, tm), BF16), pltpu.VMEM((Dm, tm), F32)],
        compiler_params=_params(("parallel", "arbitrary")),
        name="peer_dense",
    )(x, u, vt, th, e1, s2, e2, g, b)


def peer(x, w_q, keys, u, vt, g, b):
    q = linear(x, w_q)
    th, e1, s2, e2 = peer_route(q, keys)
    return peer_dense(x, u, vt, th, e1, s2, e2, g, b)


def kernel(x_prompt, x_sample, cache_win_k, cache_win_v, state_conv, cache_mem_k, cache_mem_v, mem_prompt,
           attn_w_qkv, attn_sinks, attn_w_o, conv_w_in, conv_w, conv_w_out, mem_w_q, mem_w_kv, mem_w_o,
           peer_w_q, peer_sub_keys, peer_u, peer_v, ln_g, ln_b):
    B, L, D = x_prompt.shape
    Bs, Ls, _ = x_sample.shape
    n_mem = mem_prompt.shape[1]
    nq = N_HEADS * HEAD_DIM
    nk = N_KV_HEADS * HEAD_DIM
    xp = x_prompt.reshape(B * L, D)
    xs = x_sample.reshape(Bs * Ls, D)
    mem_flat = mem_prompt.reshape(B * n_mem, D)

    wk_p, wv_p, cv_p, mk_p, mv_p, wk_s, wv_s, cv_s = [], [], [], [], [], [], [], []
    for i in range(DEPTH):
        j = i // 2
        g = ln_g[i][:, None, :]
        b = ln_b[i][:, None, :]
        if i % 2 == 0:
            w_qkv = attn_w_qkv[j].astype(BF16)
            w_o = attn_w_o[j].astype(BF16)
            sinks = attn_sinks[j]
            qkv = linear(xp, w_qkv).reshape(B, L, nq + 2 * nk)
            q = qkv[..., :nq].reshape(B, L, N_KV_HEADS, GROUP, HEAD_DIM).transpose(0, 2, 3, 1, 4)
            k = qkv[..., nq:nq + nk].reshape(B, L, N_KV_HEADS, HEAD_DIM)
            v = qkv[..., nq + nk:].reshape(B, L, N_KV_HEADS, HEAD_DIM)
            o = window_attention_prompt(q, k.transpose(0, 2, 1, 3), v.transpose(0, 2, 1, 3), sinks)
            o = o.transpose(0, 3, 1, 2, 4).reshape(B * L, nq)
            wk_p.append(k[:, L - WINDOW:])
            wv_p.append(v[:, L - WINDOW:])
            xp = linear_deepnorm(o, w_o, xp, g[0], b[0])
            qkv = linear(xs, w_qkv).reshape(Bs, Ls, nq + 2 * nk)
            q = qkv[..., :nq].reshape(Bs, Ls, N_KV_HEADS, GROUP, HEAD_DIM).transpose(0, 2, 3, 1, 4)
            q = q.reshape(Bs, N_KV_HEADS, GROUP * Ls, HEAD_DIM)
            k = qkv[..., nq:nq + nk].reshape(Bs, Ls, N_KV_HEADS, HEAD_DIM)
            v = qkv[..., nq + nk:].reshape(Bs, Ls, N_KV_HEADS, HEAD_DIM)
            kk = jnp.concatenate([cache_win_k[j], k], axis=1)
            vv = jnp.concatenate([cache_win_v[j], v], axis=1)
            pad = ((0, 0), (0, SAMPLE_KEYS - kk.shape[1]), (0, 0), (0, 0))
            sinks_col = jnp.repeat(sinks.reshape(N_KV_HEADS, GROUP), Ls, axis=1)[None, :, :, None]
            sinks_col = jnp.broadcast_to(sinks_col, (SAMPLE_BATCH_TILE, N_KV_HEADS, GROUP * Ls, 1))
            sinks_col = sinks_col.reshape(SAMPLE_BATCH_TILE * N_KV_HEADS, GROUP * Ls, 1)
            o = window_attention_sample(q, jnp.pad(kk, pad).transpose(0, 2, 1, 3),
                                        jnp.pad(vv, pad).transpose(0, 2, 1, 3), sinks_col, Ls)
            o = o.reshape(Bs, N_KV_HEADS, GROUP, Ls, HEAD_DIM).transpose(0, 3, 1, 2, 4).reshape(Bs * Ls, nq)
            wk_s.append(kk[:, Ls:])
            wv_s.append(vv[:, Ls:])
            xs = linear_deepnorm(o, w_o, xs, g[0], b[0])
        else:
            w_in = conv_w_in[j].astype(BF16)
            w_out = conv_w_out[j].astype(BF16)
            wc = conv_w[j]
            nst = wc.shape[0] - 1
            gb, u = conv_in(xp, w_in)
            C = u.shape[-1]
            up = jnp.pad(u.reshape(B, L, C), ((0, 0), (nst, 0), (0, 0)))
            taps = [up[:, t:t + L].reshape(B * L, C) for t in range(nst + 1)]
            cv_p.append(up[:, L:])
            xp = conv_out(taps[0], taps[1], taps[2], gb, wc, w_out, xp, g[0], b[0])
            gb, u = conv_in(xs, w_in)
            up = jnp.concatenate([state_conv[j], u.reshape(Bs, Ls, C)], axis=1)
            taps = [up[:, t:t + Ls].reshape(Bs * Ls, C) for t in range(nst + 1)]
            cv_s.append(up[:, Ls:])
            xs = conv_out(taps[0], taps[1], taps[2], gb, wc, w_out, xs, g[0], b[0])

        w_q = mem_w_q[i].astype(BF16)
        w_o = mem_w_o[i].astype(BF16)
        mkv = linear(mem_flat, mem_w_kv[i].astype(BF16))
        mk = mkv[:, :D].reshape(B, n_mem, D)
        mv = mkv[:, D:].reshape(B, n_mem, D)
        mk_p.append(mk.reshape(B, n_mem, MEM_HEADS, MEM_HEAD_DIM))
        mv_p.append(mv.reshape(B, n_mem, MEM_HEADS, MEM_HEAD_DIM))
        o = mem_attention(linear(xp, w_q).reshape(B, L, D), mk, mv, 1, TOKEN_TILE)
        xp = linear_deepnorm(o.reshape(B * L, D), w_o, xp, g[1], b[1])
        o = mem_attention(linear(xs, w_q).reshape(Bs, Ls, D), cache_mem_k[i].reshape(Bs, n_mem, D),
                          cache_mem_v[i].reshape(Bs, n_mem, D), SAMPLE_BATCH_TILE, Ls)
        xs = linear_deepnorm(o.reshape(Bs * Ls, D), w_o, xs, g[1], b[1])

        w_pq = peer_w_q[i].astype(BF16)
        keys = peer_sub_keys[i].astype(BF16)
        u_tab = peer_u[i].astype(BF16)
        vt_tab = peer_v[i].T.astype(BF16)
        xp = peer(xp, w_pq, keys, u_tab, vt_tab, g[2], b[2])
        xs = peer(xs, w_pq, keys, u_tab, vt_tab, g[2], b[2])

    return (xp.reshape(B, L, D), xs.reshape(Bs, Ls, D),
            jnp.stack(wk_p), jnp.stack(wv_p), jnp.stack(cv_p), jnp.stack(mk_p), jnp.stack(mv_p),
            jnp.stack(wk_s), jnp.stack(wv_s), jnp.stack(cv_s))
```
